```python
import jax, jax.numpy as jnp
from jax import lax
import numpy as np

D_MODEL = 1024
BATCH = 8
SEQ = 8192
DEPTH = 1
DEC_BATCH = 8
DEC_SEQ = 32
PAST_LEN = 1024

CHUNK = 64
D_LRU = 512
LRU_HEADS = 8
LRU_HEAD_DIM = D_LRU // LRU_HEADS
CONV_W = 4
LRU_C = 8.0
D_SG = 512
SG_GROUPS = 4
SG_GROUP_DIM = D_SG // SG_GROUPS
SG_LEN = 128
D_MIX = D_LRU + D_SG
D_IN = 2 * D_LRU + 2 * D_SG
D_FF = 2816
N_MOD = 9
ALPHA = (2 * DEPTH) ** 0.25
BETA = (8 * DEPTH) ** -0.25
LN_EPS = 1e-5

kernel_name = "hybrid_rglru_gmlp_streaming_step"


def layer_norm(x, g, b):
    xf = x.astype(jnp.float32)
    mu = jnp.mean(xf, axis=-1, keepdims=True)
    xc = xf - mu
    var = jnp.mean(xc * xc, axis=-1, keepdims=True)
    return (xc * lax.rsqrt(var + LN_EPS) * g.astype(jnp.float32) + b.astype(jnp.float32)).astype(x.dtype)


def swiglu(h, w1, w3, w2):
    return (jax.nn.silu(h @ w1) * (h @ w3)) @ w2


def causal_conv(x, hist, w, b):
    T = x.shape[1]
    xp = jnp.concatenate([hist.astype(x.dtype), x], axis=1)
    y = sum((w[k] * xp[:, k:k + T] for k in range(CONV_W)), b)
    return y, xp[:, -(CONV_W - 1):]


def rg_lru(x, h0, w_a, b_a, w_x, b_x, lam):
    B, T, _ = x.shape
    xh = x.reshape(B, T, LRU_HEADS, LRU_HEAD_DIM)
    r = jax.nn.sigmoid(jnp.einsum('bthi,hij->bthj', xh, w_a).reshape(B, T, D_LRU) + b_a)
    i = jax.nn.sigmoid(jnp.einsum('bthi,hij->bthj', xh, w_x).reshape(B, T, D_LRU) + b_x)
    log_a = -LRU_C * r.astype(jnp.float32) * jax.nn.softplus(-lam.astype(jnp.float32))
    a = jnp.exp(log_a)
    u = jnp.sqrt(-jnp.expm1(2.0 * log_a)) * (i * x).astype(jnp.float32)

    def combine(left, right):
        a1, b1 = left
        a2, b2 = right
        return a1 * a2, a2 * b1 + b2

    a_cum, h_cum = lax.associative_scan(combine, (a, u), axis=1)
    h = h_cum + a_cum * h0[:, None, :].astype(jnp.float32)
    return h.astype(x.dtype), h[:, -1].astype(x.dtype)


def spatial_gate(u, v, w_s, b_s):
    B, T, _ = v.shape
    n = -(-T // SG_LEN)
    pad = n * SG_LEN - T
    vp = jnp.pad(v, ((0, 0), (0, pad), (0, 0))).reshape(B, n, SG_LEN, SG_GROUPS, SG_GROUP_DIM)
    mask = jnp.tril(jnp.ones((SG_LEN, SG_LEN), dtype=bool))
    w = jnp.where(mask[None], w_s, jnp.zeros_like(w_s))
    z = jnp.einsum('gij,bcjgd->bcigd', w, vp) + jnp.transpose(b_s)[None, None, :, :, None]
    z = z.reshape(B, n * SG_LEN, D_SG)[:, :T]
    return u * z


def layer(x, c, conv_hist, h0, w_ada, b_ada, ln_g, ln_b,
          ffn1_w1, ffn1_w3, ffn1_w2, ffn2_w1, ffn2_w3, ffn2_w2,
          w_in, w_out, conv_w, conv_b, lru_wa, lru_ba, lru_wx, lru_bx, lru_lambda,
          sg_ln_g, sg_ln_b, sg_w, sg_b):
    B = x.shape[0]
    mod = (jax.nn.silu(c) @ w_ada + b_ada).reshape(B, N_MOD, D_MODEL)[:, None]

    def modulate(h, k):
        return h * (1.0 + mod[:, :, 3 * k + 1]) + mod[:, :, 3 * k]

    def gate(k):
        return 1.0 + mod[:, :, 3 * k + 2]

    f = swiglu(modulate(x, 0), ffn1_w1, ffn1_w3, ffn1_w2)
    x = layer_norm(ALPHA * x + 0.5 * gate(0) * f, ln_g[0], ln_b[0])

    proj = modulate(x, 1) @ w_in
    xa, ga, us, vs = jnp.split(proj, [D_LRU, 2 * D_LRU, 2 * D_LRU + D_SG], axis=-1)
    xc, new_hist = causal_conv(xa, conv_hist, conv_w, conv_b)
    hl, h_last = rg_lru(xc, h0, lru_wa, lru_ba, lru_wx, lru_bx, lru_lambda)
    ya = hl * jax.nn.gelu(ga)
    vn = layer_norm(vs, sg_ln_g, sg_ln_b)
    yb = spatial_gate(us, vn, sg_w, sg_b)
    m = jnp.concatenate([ya, yb], axis=-1) @ w_out
    x = layer_norm(ALPHA * x + gate(1) * m, ln_g[1], ln_b[1])

    f = swiglu(modulate(x, 2), ffn2_w1, ffn2_w3, ffn2_w2)
    x = layer_norm(ALPHA * x + 0.5 * gate(2) * f, ln_g[2], ln_b[2])
    return x, new_hist, h_last, vn


def setup_inputs(seed: int = 0) -> dict:
    key = jax.random.key(seed)
    ks = iter(jax.random.split(key, 40))
    nrm = lambda shape, s: jax.random.normal(next(ks), shape, jnp.float32) * s
    a0 = jax.random.uniform(next(ks), (DEPTH, D_LRU), jnp.float32, 0.9, 0.999)
    sig = a0 ** (1.0 / LRU_C)
    lru_lambda = jnp.log(sig) - jnp.log1p(-sig)
    return {
        "x_prompt": nrm((BATCH, SEQ, D_MODEL), 1.0),
        "x_sample": nrm((DEC_BATCH, DEC_SEQ, D_MODEL), 1.0),
        "c_prompt": nrm((BATCH, D_MODEL), 1.0),
        "c_sample": nrm((DEC_BATCH, D_MODEL), 1.0),
        "state_conv": nrm((DEPTH, DEC_BATCH, CONV_W - 1, D_LRU), 1.0),
        "state_lru": nrm((DEPTH, DEC_BATCH, D_LRU), 0.5),
        "ln_in_g": 1.0 + nrm((D_MODEL,), 0.02),
        "ln_in_b": nrm((D_MODEL,), 0.02),
        "w_ada": nrm((DEPTH, D_MODEL, N_MOD * D_MODEL), 0.2 * D_MODEL ** -0.5),
        "b_ada": nrm((DEPTH, N_MOD * D_MODEL), 0.02),
        "ln_g": 1.0 + nrm((DEPTH, 3, D_MODEL), 0.02),
        "ln_b": nrm((DEPTH, 3, D_MODEL), 0.02),
        "ffn1_w1": nrm((DEPTH, D_MODEL, D_FF), BETA * D_MODEL ** -0.5),
        "ffn1_w3": nrm((DEPTH, D_MODEL, D_FF), BETA * D_MODEL ** -0.5),
        "ffn1_w2": nrm((DEPTH, D_FF, D_MODEL), BETA * D_FF ** -0.5),
        "ffn2_w1": nrm((DEPTH, D_MODEL, D_FF), BETA * D_MODEL ** -0.5),
        "ffn2_w3": nrm((DEPTH, D_MODEL, D_FF), BETA * D_MODEL ** -0.5),
        "ffn2_w2": nrm((DEPTH, D_FF, D_MODEL), BETA * D_FF ** -0.5),
        "w_in": nrm((DEPTH, D_MODEL, D_IN), D_MODEL ** -0.5),
        "w_out": nrm((DEPTH, D_MIX, D_MODEL), BETA * D_MIX ** -0.5),
        "conv_w": nrm((DEPTH, CONV_W, D_LRU), CONV_W ** -0.5),
        "conv_b": nrm((DEPTH, D_LRU), 0.02),
        "lru_wa": nrm((DEPTH, LRU_HEADS, LRU_HEAD_DIM, LRU_HEAD_DIM), LRU_HEAD_DIM ** -0.5),
        "lru_ba": nrm((DEPTH, D_LRU), 0.02),
        "lru_wx": nrm((DEPTH, LRU_HEADS, LRU_HEAD_DIM, LRU_HEAD_DIM), LRU_HEAD_DIM ** -0.5),
        "lru_bx": nrm((DEPTH, D_LRU), 0.02),
        "lru_lambda": lru_lambda,
        "sg_ln_g": 1.0 + nrm((DEPTH, D_SG), 0.02),
        "sg_ln_b": nrm((DEPTH, D_SG), 0.02),
        "sg_w": nrm((DEPTH, SG_GROUPS, SG_LEN, SG_LEN), SG_LEN ** -0.5),
        "sg_b": 1.0 + nrm((DEPTH, SG_GROUPS, SG_LEN), 0.02),
    }


def reference(x_prompt, x_sample, c_prompt, c_sample, state_conv, state_lru,
              ln_in_g, ln_in_b, w_ada, b_ada, ln_g, ln_b,
              ffn1_w1, ffn1_w3, ffn1_w2, ffn2_w1, ffn2_w3, ffn2_w2,
              w_in, w_out, conv_w, conv_b, lru_wa, lru_ba, lru_wx, lru_bx, lru_lambda,
              sg_ln_g, sg_ln_b, sg_w, sg_b):
    def run(x, c, conv_hists, h0s):
        x = layer_norm(x, ln_in_g, ln_in_b)
        hists, hs, vs = [], [], []
        for l in range(DEPTH):
            x, nh, hl, vn = layer(
                x, c, conv_hists[l], h0s[l], w_ada[l], b_ada[l], ln_g[l], ln_b[l],
                ffn1_w1[l], ffn1_w3[l], ffn1_w2[l], ffn2_w1[l], ffn2_w3[l], ffn2_w2[l],
                w_in[l], w_out[l], conv_w[l], conv_b[l], lru_wa[l], lru_ba[l],
                lru_wx[l], lru_bx[l], lru_lambda[l], sg_ln_g[l], sg_ln_b[l], sg_w[l], sg_b[l])
            hists.append(nh)
            hs.append(hl)
            vs.append(vn)
        return x, jnp.stack(hists), jnp.stack(hs), jnp.stack(vs)

    Bp = x_prompt.shape[0]
    zero_hist = jnp.zeros((DEPTH, Bp, CONV_W - 1, D_LRU), x_prompt.dtype)
    zero_h = jnp.zeros((DEPTH, Bp, D_LRU), x_prompt.dtype)
    y_prompt, conv_p, lru_p, _ = run(x_prompt, c_prompt, zero_hist, zero_h)
    y_sample, conv_s, lru_s, v_s = run(x_sample, c_sample, state_conv, state_lru)
    return (y_prompt, y_sample, conv_p, lru_p, conv_s, lru_s, v_s)
```

```python
import functools
import math

import jax
import jax.numpy as jnp
from jax import lax
from jax.experimental import pallas as pl
from jax.experimental.pallas import tpu as pltpu

D_MODEL = 1024
DEPTH = 1
CONV_W = 4
D_LRU = 512
LRU_HEADS = 8
LRU_HEAD_DIM = D_LRU // LRU_HEADS
LRU_C = 8.0
D_SG = 512
SG_GROUPS = 4
SG_GROUP_DIM = D_SG // SG_GROUPS
SG_LEN = 128
D_IN = 2 * D_LRU + 2 * D_SG
D_FF = 2816
N_MOD = 9
ALPHA = (2 * DEPTH) ** 0.25
LN_EPS = 1e-5

SUBLANES = 8
MXU_DIM = 256
VMEM_LIMIT_BYTES = 56 * 1024 * 1024

F32 = jnp.float32
BF16 = jnp.bfloat16


def _layer_norm(x, g, b):
    mu = jnp.mean(x, axis=-1, keepdims=True)
    xc = x - mu
    var = jnp.mean(xc * xc, axis=-1, keepdims=True)
    return xc * lax.rsqrt(var + LN_EPS) * g + b


def _sigmoid(x):
    return 1.0 / (1.0 + jnp.exp(-x))


def _dot(a, b):
    return jnp.dot(a, b, preferred_element_type=F32)


def _resident(shape):
    zeros = (0,) * len(shape)
    return pl.BlockSpec(shape, lambda *_: zeros, pipeline_mode=pl.Buffered(1))


def _mod_kernel(c_ref, w_ref, b_ref, o_ref):
    c = c_ref[...]
    s = (c * _sigmoid(c)).astype(BF16)
    o_ref[...] = _dot(s, w_ref[...].astype(BF16)) + b_ref[...]


def _mod_call(c, w_ada, b_ada):
    n_rows = c.shape[0]
    n_out = w_ada.shape[1]
    tn = D_MODEL
    return pl.pallas_call(
        _mod_kernel,
        grid=(n_out // tn,),
        in_specs=[
            pl.BlockSpec((n_rows, D_MODEL), lambda n: (0, 0)),
            pl.BlockSpec((D_MODEL, tn), lambda n: (0, n)),
            pl.BlockSpec((1, tn), lambda n: (0, n)),
        ],
        out_specs=pl.BlockSpec((n_rows, tn), lambda n: (0, n)),
        out_shape=jax.ShapeDtypeStruct((n_rows, n_out), F32),
        compiler_params=pltpu.CompilerParams(dimension_semantics=("arbitrary",)),
        name="adaln_mod",
    )(c, w_ada, b_ada.reshape(1, n_out))


def _ffn_kernel(x_ref, mod_ref, lnp_ref, w1_ref, w3_ref, w2_ref, o_ref, *, k, pre_ln):
    x = x_ref[...]
    if pre_ln:
        x = _layer_norm(x, lnp_ref[0:1, :], lnp_ref[1:2, :])
    shift = mod_ref[3 * k:3 * k + 1, :]
    scale = mod_ref[3 * k + 1:3 * k + 2, :]
    gate = mod_ref[3 * k + 2:3 * k + 3, :]
    h = (x * (1.0 + scale) + shift).astype(BF16)
    h1 = _dot(h, w1_ref[...])
    h3 = _dot(h, w3_ref[...])
    g = (h1 * _sigmoid(h1) * h3).astype(BF16)
    f = _dot(g, w2_ref[...])
    y = ALPHA * x + (0.5 * (1.0 + gate)) * f
    o_ref[...] = _layer_norm(y, lnp_ref[2:3, :], lnp_ref[3:4, :])


def _ffn_call(x, mod, lnp, w1, w3, w2, *, k, pre_ln, tm):
    B, T, _ = x.shape
    return pl.pallas_call(
        functools.partial(_ffn_kernel, k=k, pre_ln=pre_ln),
        grid=(B, T // tm),
        in_specs=[
            pl.BlockSpec((None, tm, D_MODEL), lambda b, j: (b, j, 0)),
            pl.BlockSpec((None, N_MOD, D_MODEL), lambda b, j: (b, 0, 0)),
            _resident((4, D_MODEL)),
            _resident((D_MODEL, D_FF)),
            _resident((D_MODEL, D_FF)),
            _resident((D_FF, D_MODEL)),
        ],
        out_specs=pl.BlockSpec((None, tm, D_MODEL), lambda b, j: (b, j, 0)),
        out_shape=jax.ShapeDtypeStruct((B, T, D_MODEL), F32),
        compiler_params=pltpu.CompilerParams(
            dimension_semantics=("arbitrary", "arbitrary"),
            vmem_limit_bytes=VMEM_LIMIT_BYTES),
        name=f"swiglu_half_step_{k}",
    )(x, mod, lnp, w1, w3, w2)


def _mixer_kernel(x_ref, mod_ref, lnp_ref, win_ref, wout_ref, convw_ref, convb_ref,
                  wa_ref, ba_ref, wx_ref, bx_ref, lam_ref, sgg_ref, sgb_ref, sgw_ref, sgz_ref,
                  hist0_ref, h0_ref, *rest, tm, emit_vn):
    if emit_vn:
        y_ref, hist_ref, hlast_ref, vn_ref, xa_ext, h_carry = rest
    else:
        y_ref, hist_ref, hlast_ref, xa_ext, h_carry = rest
    j = pl.program_id(1)

    @pl.when(j == 0)
    def _():
        xa_ext[0:SUBLANES, :] = jnp.zeros((SUBLANES, D_LRU), F32)
        xa_ext[SUBLANES - (CONV_W - 1):SUBLANES, :] = hist0_ref[...]
        h_carry[...] = jnp.broadcast_to(h0_ref[...], (SUBLANES, D_LRU))

    x = x_ref[...]
    shift = mod_ref[3:4, :]
    scale = mod_ref[4:5, :]
    gate = mod_ref[5:6, :]
    hm = (x * (1.0 + scale) + shift).astype(BF16)
    proj = _dot(hm, win_ref[...])
    xa = proj[:, 0:D_LRU]
    ga = proj[:, D_LRU:2 * D_LRU]
    us = proj[:, 2 * D_LRU:2 * D_LRU + D_SG]
    vs = proj[:, 2 * D_LRU + D_SG:]

    xa_ext[SUBLANES:SUBLANES + tm, :] = xa
    xc = convb_ref[...] + convw_ref[CONV_W - 1:CONV_W, :] * xa
    for d in range(1, CONV_W):
        xc = xc + convw_ref[CONV_W - 1 - d:CONV_W - d, :] * xa_ext[SUBLANES - d:SUBLANES - d + tm, :]
    hist_ref[...] = xa_ext[SUBLANES + tm - (CONV_W - 1):SUBLANES + tm, :]
    xa_ext[0:SUBLANES, :] = xa_ext[tm:tm + SUBLANES, :]

    xcb = xc.astype(BF16)
    halves = []
    for w_ref in (wa_ref, wx_ref):
        halves.append(jnp.concatenate(
            [_dot(xcb[:, 0:MXU_DIM], w_ref[0:MXU_DIM, 0:MXU_DIM]),
             _dot(xcb[:, MXU_DIM:], w_ref[MXU_DIM:, MXU_DIM:])], axis=1))
    r = _sigmoid(halves[0] + ba_ref[...])
    i = _sigmoid(halves[1] + bx_ref[...])
    lam = lam_ref[...]
    softplus_neg_lam = jnp.maximum(-lam, 0.0) + jnp.log1p(jnp.exp(-jnp.abs(lam)))
    log_a = (-LRU_C) * r * softplus_neg_lam
    a = jnp.exp(log_a)
    t = jnp.tanh(log_a)
    u = jnp.sqrt((-2.0 * t) / (1.0 - t)) * (i * xc)

    row = lax.broadcasted_iota(jnp.int32, (SUBLANES, D_LRU), 0)
    hb = h_carry[...]
    groups = []
    for g in range(tm // SUBLANES):
        A = a[g * SUBLANES:(g + 1) * SUBLANES, :]
        U = u[g * SUBLANES:(g + 1) * SUBLANES, :]
        for s in (1, 2, 4):
            keep = row >= s
            a_prev = jnp.where(keep, pltpu.roll(A, s, 0), 1.0)
            u_prev = jnp.where(keep, pltpu.roll(U, s, 0), 0.0)
            U = A * u_prev + U
            A = A * a_prev
        H = U + A * hb
        groups.append(H)
        hb = jnp.broadcast_to(H[SUBLANES - 1:SUBLANES, :], (SUBLANES, D_LRU))
    h_carry[...] = hb
    hlast_ref[...] = hb[0:1, :]
    hl = jnp.concatenate(groups, axis=0)

    gelu_ga = 0.5 * ga * (1.0 + jnp.tanh(math.sqrt(2.0 / math.pi) * (ga + 0.044715 * (ga * ga * ga))))
    ya = hl * gelu_ga

    vn = _layer_norm(vs, sgg_ref[...], sgb_ref[...])
    if emit_vn:
        vn_ref[...] = vn
    wrow = lax.broadcasted_iota(jnp.int32, (SG_LEN, SG_LEN), 0)
    wcol = lax.broadcasted_iota(jnp.int32, (SG_LEN, SG_LEN), 1)
    w_causal = [jnp.where(wrow >= wcol, sgw_ref[g], 0.0).astype(BF16) for g in range(SG_GROUPS)]
    vnb = vn.astype(BF16)
    if tm < SG_LEN:
        vnb = jnp.concatenate([vnb, jnp.zeros((SG_LEN - tm, D_SG), BF16)], axis=0)
    z_chunks = []
    for c in range(max(tm // SG_LEN, 1)):
        vc = vnb[c * SG_LEN:(c + 1) * SG_LEN, :]
        zc = jnp.concatenate(
            [_dot(w_causal[g], vc[:, g * SG_GROUP_DIM:(g + 1) * SG_GROUP_DIM]) for g in range(SG_GROUPS)],
            axis=1) + sgz_ref[...]
        z_chunks.append(zc[0:min(tm, SG_LEN), :])
    z = z_chunks[0] if len(z_chunks) == 1 else jnp.concatenate(z_chunks, axis=0)
    yb = us * z

    m = _dot(jnp.concatenate([ya, yb], axis=1).astype(BF16), wout_ref[...])
    y = ALPHA * x + (1.0 + gate) * m
    y_ref[...] = _layer_norm(y, lnp_ref[0:1, :], lnp_ref[1:2, :])


def _mixer_call(x, mod, lnp, win, wout, convw, convb, wa, ba, wx, bx, lam, sgg, sgb, sgw, sgz,
                hist0, h0, *, tm, emit_vn):
    B, T, _ = x.shape
    per_batch = lambda b, j: (b, 0, 0)
    tile = lambda b, j: (b, j, 0)
    out_shape = [
        jax.ShapeDtypeStruct((B, T, D_MODEL), F32),
        jax.ShapeDtypeStruct((B, CONV_W - 1, D_LRU), F32),
        jax.ShapeDtypeStruct((B, 1, D_LRU), F32),
    ]
    out_specs = [
        pl.BlockSpec((None, tm, D_MODEL), tile),
        pl.BlockSpec((None, CONV_W - 1, D_LRU), per_batch),
        pl.BlockSpec((None, 1, D_LRU), per_batch),
    ]
    if emit_vn:
        out_shape.append(jax.ShapeDtypeStruct((B, T, D_SG), F32))
        out_specs.append(pl.BlockSpec((None, tm, D_SG), tile))
    return pl.pallas_call(
        functools.partial(_mixer_kernel, tm=tm, emit_vn=emit_vn),
        grid=(B, T // tm),
        in_specs=[
            pl.BlockSpec((None, tm, D_MODEL), tile),
            pl.BlockSpec((None, N_MOD, D_MODEL), per_batch),
            _resident((2, D_MODEL)),
            _resident((D_MODEL, D_IN)),
            _resident((D_LRU + D_SG, D_MODEL)),
            _resident((CONV_W, D_LRU)),
            _resident((1, D_LRU)),
            _resident((D_LRU, D_LRU)),
            _resident((1, D_LRU)),
            _resident((D_LRU, D_LRU)),
            _resident((1, D_LRU)),
            _resident((1, D_LRU)),
            _resident((1, D_SG)),
            _resident((1, D_SG)),
            _resident((SG_GROUPS, SG_LEN, SG_LEN)),
            _resident((SG_LEN, D_SG)),
            pl.BlockSpec((None, CONV_W - 1, D_LRU), per_batch),
            pl.BlockSpec((None, 1, D_LRU), per_batch),
        ],
        out_specs=out_specs,
        out_shape=out_shape,
        scratch_shapes=[
            pltpu.VMEM((tm + SUBLANES, D_LRU), F32),
            pltpu.VMEM((SUBLANES, D_LRU), F32),
        ],
        compiler_params=pltpu.CompilerParams(
            dimension_semantics=("arbitrary", "arbitrary"),
            vmem_limit_bytes=VMEM_LIMIT_BYTES),
        name="rglru_gmlp_mixer",
    )(x, mod, lnp, win, wout, convw, convb, wa, ba, wx, bx, lam, sgg, sgb, sgw, sgz, hist0, h0)


def _block_diag(w):
    n_heads, d, _ = w.shape
    eye = jnp.eye(n_heads, dtype=w.dtype)
    return (eye[:, None, :, None] * w[:, :, None, :]).reshape(n_heads * d, n_heads * d)


def _tile_rows(T):
    for tm in (256, 128):
        if T % tm == 0:
            return tm
    return T


def kernel(x_prompt, x_sample, c_prompt, c_sample, state_conv, state_lru, ln_in_g, ln_in_b, w_ada, b_ada, ln_g, ln_b, ffn1_w1, ffn1_w3, ffn1_w2, ffn2_w1, ffn2_w3, ffn2_w2, w_in, w_out, conv_w, conv_b, lru_wa, lru_ba, lru_wx, lru_bx, lru_lambda, sg_ln_g, sg_ln_b, sg_w, sg_b):
    Bp = x_prompt.shape[0]
    row = lambda v: v.reshape(1, -1)
    c_all = jnp.concatenate([c_prompt, c_sample], axis=0)

    def run(x, mods, hists, h0s, emit_vn):
        tm = _tile_rows(x.shape[1])
        new_hists, new_hs, vns = [], [], []
        for l in range(DEPTH):
            mod = mods[l]
            lnp1 = jnp.stack([ln_in_g, ln_in_b, ln_g[l, 0], ln_b[l, 0]])
            lnp2 = jnp.stack([ln_g[l, 1], ln_b[l, 1]])
            lnp3 = jnp.stack([ln_g[l, 1], ln_b[l, 1], ln_g[l, 2], ln_b[l, 2]])
            x = _ffn_call(x, mod, lnp1, ffn1_w1[l].astype(BF16), ffn1_w3[l].astype(BF16),
                          ffn1_w2[l].astype(BF16), k=0, pre_ln=(l == 0), tm=tm)
            sgz = jnp.repeat(jnp.transpose(sg_b[l]), SG_GROUP_DIM, axis=1)
            outs = _mixer_call(
                x, mod, lnp2, w_in[l].astype(BF16), w_out[l].astype(BF16), conv_w[l], row(conv_b[l]),
                _block_diag(lru_wa[l]).astype(BF16), row(lru_ba[l]),
                _block_diag(lru_wx[l]).astype(BF16), row(lru_bx[l]), row(lru_lambda[l]),
                row(sg_ln_g[l]), row(sg_ln_b[l]), sg_w[l], sgz,
                hists[l], h0s[l][:, None, :], tm=tm, emit_vn=emit_vn)
            x = outs[0]
            new_hists.append(outs[1])
            new_hs.append(outs[2][:, 0, :])
            if emit_vn:
                vns.append(outs[3])
            x = _ffn_call(x, mod, lnp3, ffn2_w1[l].astype(BF16), ffn2_w3[l].astype(BF16),
                          ffn2_w2[l].astype(BF16), k=2, pre_ln=False, tm=tm)
        return x, jnp.stack(new_hists), jnp.stack(new_hs), (jnp.stack(vns) if emit_vn else None)

    mods = [_mod_call(c_all, w_ada[l], b_ada[l]).reshape(-1, N_MOD, D_MODEL) for l in range(DEPTH)]
    mods_p = [m[:Bp] for m in mods]
    mods_s = [m[Bp:] for m in mods]
    zero_hist = jnp.zeros((DEPTH, Bp, CONV_W - 1, D_LRU), x_prompt.dtype)
    zero_h = jnp.zeros((DEPTH, Bp, D_LRU), x_prompt.dtype)
    y_prompt, conv_p, lru_p, _ = run(x_prompt, mods_p, zero_hist, zero_h, False)
    y_sample, conv_s, lru_s, v_s = run(x_sample, mods_s, state_conv, state_lru, True)
    return (y_prompt, y_sample, conv_p, lru_p, conv_s, lru_s, v_s)
```

```python
import functools
import math

import jax
import jax.numpy as jnp
from jax import lax
from jax.experimental import pallas as pl
from jax.experimental.pallas import tpu as pltpu

D_MODEL = 1024
DEPTH = 1
CONV_W = 4
D_LRU = 512
LRU_HEADS = 8
LRU_HEAD_DIM = D_LRU // LRU_HEADS
LRU_C = 8.0
D_SG = 512
SG_GROUPS = 4
SG_GROUP_DIM = D_SG // SG_GROUPS
SG_LEN = 128
D_IN = 2 * D_LRU + 2 * D_SG
D_FF = 2816
N_MOD = 9
ALPHA = (2 * DEPTH) ** 0.25
LN_EPS = 1e-5

SUBLANES = 8
LANES = 128
MXU_DIM = 256
VMEM_LIMIT_BYTES = 56 * 1024 * 1024

FFN_TILE_ROWS = 1024
MIXER_TILE_ROWS = 512
SUB_ROWS = 256

F32 = jnp.float32
BF16 = jnp.bfloat16


def _layer_norm(x, g, b):
    mu = jnp.mean(x, axis=-1, keepdims=True)
    xc = x - mu
    var = jnp.mean(xc * xc, axis=-1, keepdims=True)
    return xc * lax.rsqrt(var + LN_EPS) * g + b


def _sigmoid(x):
    return 1.0 / (1.0 + jnp.exp(-x))


def _dot(a, b):
    return jnp.dot(a, b, preferred_element_type=F32)


def _resident(shape):
    zeros = (0,) * len(shape)
    return pl.BlockSpec(shape, lambda *_: zeros, pipeline_mode=pl.Buffered(1))


def _tile_rows(T, want):
    return want if T % want == 0 else T


def _sub_rows(tm):
    return SUB_ROWS if tm % SUB_ROWS == 0 else tm


def _mod_kernel(c_ref, w_ref, b_ref, o_ref):
    c = c_ref[...]
    s = (c * _sigmoid(c)).astype(BF16)
    o_ref[...] = _dot(s, w_ref[...].astype(BF16)) + b_ref[...]


def _mod_call(c, w_ada, b_ada):
    n_rows = c.shape[0]
    n_out = w_ada.shape[1]
    tn = D_MODEL
    return pl.pallas_call(
        _mod_kernel,
        grid=(n_out // tn,),
        in_specs=[
            pl.BlockSpec((n_rows, D_MODEL), lambda n: (0, 0)),
            pl.BlockSpec((D_MODEL, tn), lambda n: (0, n)),
            pl.BlockSpec((1, tn), lambda n: (0, n)),
        ],
        out_specs=pl.BlockSpec((n_rows, tn), lambda n: (0, n)),
        out_shape=jax.ShapeDtypeStruct((n_rows, n_out), F32),
        compiler_params=pltpu.CompilerParams(dimension_semantics=("arbitrary",)),
        name="adaln_mod",
    )(c, w_ada, b_ada.reshape(1, n_out))


def _ffn_kernel(x_ref, mod_ref, lnp_ref, w1_ref, w3_ref, w2_ref, o_ref, *, k, pre_ln):
    shift = mod_ref[3 * k:3 * k + 1, :]
    scale = mod_ref[3 * k + 1:3 * k + 2, :]
    gate = mod_ref[3 * k + 2:3 * k + 3, :]
    tm = x_ref.shape[0]
    sub = _sub_rows(tm)
    for s in range(tm // sub):
        x = x_ref[s * sub:(s + 1) * sub, :]
        if pre_ln:
            x = _layer_norm(x, lnp_ref[0:1, :], lnp_ref[1:2, :])
        h = (x * (1.0 + scale) + shift).astype(BF16)
        h1 = _dot(h, w1_ref[...])
        h3 = _dot(h, w3_ref[...])
        g = (h1 * _sigmoid(h1) * h3).astype(BF16)
        f = _dot(g, w2_ref[...])
        y = ALPHA * x + (0.5 * (1.0 + gate)) * f
        o_ref[s * sub:(s + 1) * sub, :] = _layer_norm(y, lnp_ref[2:3, :], lnp_ref[3:4, :])


def _ffn_call(x, mod, lnp, w1, w3, w2, *, k, pre_ln, tm):
    B, T, _ = x.shape
    return pl.pallas_call(
        functools.partial(_ffn_kernel, k=k, pre_ln=pre_ln),
        grid=(B, T // tm),
        in_specs=[
            pl.BlockSpec((None, tm, D_MODEL), lambda b, j: (b, j, 0)),
            pl.BlockSpec((None, N_MOD, D_MODEL), lambda b, j: (b, 0, 0)),
            _resident((4, D_MODEL)),
            _resident((D_MODEL, D_FF)),
            _resident((D_MODEL, D_FF)),
            _resident((D_FF, D_MODEL)),
        ],
        out_specs=pl.BlockSpec((None, tm, D_MODEL), lambda b, j: (b, j, 0)),
        out_shape=jax.ShapeDtypeStruct((B, T, D_MODEL), F32),
        compiler_params=pltpu.CompilerParams(
            dimension_semantics=("arbitrary", "arbitrary"),
            vmem_limit_bytes=VMEM_LIMIT_BYTES),
        name=f"swiglu_half_step_{k}",
    )(x, mod, lnp, w1, w3, w2)


def _seg_pitch(seg_len):
    p = seg_len
    while p % 8 != 4:
        p += 1
    return p


def _to_segments(scr, val, seg_len, pitch):
    n_slabs = val.shape[1] // LANES
    for s in range(SUBLANES):
        for c in range(n_slabs):
            scr[c, s * pitch:s * pitch + seg_len, :] = val[s * seg_len:(s + 1) * seg_len, c * LANES:(c + 1) * LANES]
    return [jnp.concatenate([scr[c, pl.ds(tau, SUBLANES, stride=pitch), :] for c in range(n_slabs)], axis=1)
            for tau in range(seg_len)]


def _from_segments(scr, rows, seg_len, pitch):
    n_slabs = rows[0].shape[1] // LANES
    for tau in range(seg_len):
        for c in range(n_slabs):
            scr[c, pl.ds(tau, SUBLANES, stride=pitch), :] = rows[tau][:, c * LANES:(c + 1) * LANES]
    return jnp.concatenate(
        [jnp.concatenate([scr[c, s * pitch:s * pitch + seg_len, :] for c in range(n_slabs)], axis=1)
         for s in range(SUBLANES)], axis=0)


def _shift_rows(v, fill, d, row):
    return jnp.where(row >= d, pltpu.roll(v, d, 0), fill)


def _mixer_kernel(x_ref, mod_ref, lnp_ref, win_ref, wout_ref, convw_ref, convb_ref,
                  wa_ref, ba_ref, wx_ref, bx_ref, lam_ref, sgg_ref, sgb_ref, sgw_ref, sgz_ref,
                  hist0_ref, h0_ref, *rest, tm, emit_vn):
    if emit_vn:
        y_ref, hist_ref, hlast_ref, vn_ref, seg_scr, conv_carry, h_carry = rest
    else:
        y_ref, hist_ref, hlast_ref, seg_scr, conv_carry, h_carry = rest
    j = pl.program_id(1)
    sub = _sub_rows(tm)
    seg_len = sub // SUBLANES
    pitch = _seg_pitch(seg_len)

    @pl.when(j == 0)
    def _():
        conv_carry[...] = jnp.zeros(conv_carry.shape, F32)
        for d in range(1, CONV_W):
            conv_carry[d - 1, SUBLANES - 1:SUBLANES, :] = hist0_ref[CONV_W - 1 - d:CONV_W - d, :]
        h_carry[...] = jnp.broadcast_to(h0_ref[...], (SUBLANES, D_LRU))

    shift = mod_ref[3:4, :]
    scale = mod_ref[4:5, :]
    gate = mod_ref[5:6, :]
    row = lax.broadcasted_iota(jnp.int32, (SUBLANES, D_LRU), 0)
    lam = lam_ref[...]
    softplus_neg_lam = jnp.maximum(-lam, 0.0) + jnp.log1p(jnp.exp(-jnp.abs(lam)))
    wrow = lax.broadcasted_iota(jnp.int32, (SG_LEN, SG_LEN), 0)
    wcol = lax.broadcasted_iota(jnp.int32, (SG_LEN, SG_LEN), 1)
    w_causal = [jnp.where(wrow >= wcol, sgw_ref[g], 0.0).astype(BF16) for g in range(SG_GROUPS)]
    conv_w = [jnp.broadcast_to(convw_ref[k:k + 1, :], (SUBLANES, D_LRU)) for k in range(CONV_W)]
    conv_b = jnp.broadcast_to(convb_ref[...], (SUBLANES, D_LRU))

    prev_tail = [conv_carry[d - 1] for d in range(1, CONV_W)]
    hb = h_carry[...]
    xa = None
    for s_i in range(tm // sub):
        r0 = s_i * sub
        x = x_ref[r0:r0 + sub, :]
        hm = (x * (1.0 + scale) + shift).astype(BF16)
        proj = _dot(hm, win_ref[...])
        xa = proj[:, 0:D_LRU]
        ga = proj[:, D_LRU:2 * D_LRU]
        us = proj[:, 2 * D_LRU:2 * D_LRU + D_SG]
        vs = proj[:, 2 * D_LRU + D_SG:]

        xa_s = _to_segments(seg_scr.at[s_i, 0], xa, seg_len, pitch)
        ga_s = _to_segments(seg_scr.at[s_i, 1], ga, seg_len, pitch)

        before = [_shift_rows(xa_s[seg_len - d], pltpu.roll(prev_tail[d - 1], 1, 0), 1, row)
                  for d in range(1, CONV_W)]
        prev_tail = [xa_s[seg_len - d] for d in range(1, CONV_W)]

        def back(tau, d):
            return xa_s[tau - d] if tau >= d else before[d - tau - 1]

        xc_s = []
        for tau in range(seg_len):
            acc = conv_b + conv_w[CONV_W - 1] * xa_s[tau]
            for d in range(1, CONV_W):
                acc = acc + conv_w[CONV_W - 1 - d] * back(tau, d)
            xc_s.append(acc)
        xc = jnp.concatenate(xc_s, axis=0)

        xcb = xc.astype(BF16)
        halves = []
        for w_ref in (wa_ref, wx_ref):
            halves.append(jnp.concatenate(
                [_dot(xcb[:, 0:MXU_DIM], w_ref[0:MXU_DIM, 0:MXU_DIM]),
                 _dot(xcb[:, MXU_DIM:], w_ref[MXU_DIM:, MXU_DIM:])], axis=1))
        r = _sigmoid(halves[0] + ba_ref[...])
        i = _sigmoid(halves[1] + bx_ref[...])
        log_a = (-LRU_C) * r * softplus_neg_lam
        a = jnp.exp(log_a)
        t = jnp.tanh(log_a)
        v = (-2.0 * t) / (1.0 - t)
        u = jnp.where(v > 0.0, v * lax.rsqrt(v), 0.0) * (i * xc)

        h_loc, a_cum = [], []
        for tau in range(seg_len):
            a_t = a[tau * SUBLANES:(tau + 1) * SUBLANES, :]
            u_t = u[tau * SUBLANES:(tau + 1) * SUBLANES, :]
            h_loc.append(u_t if tau == 0 else a_t * h_loc[-1] + u_t)
            a_cum.append(a_t if tau == 0 else a_t * a_cum[-1])
        A, H = a_cum[-1], h_loc[-1]
        for d in (1, 2, 4):
            H = A * _shift_rows(H, 0.0, d, row) + H
            A = A * _shift_rows(A, 1.0, d, row)
        seg_end = H + A * hb
        seg_entry = _shift_rows(seg_end, hb, 1, row)
        hb = jnp.broadcast_to(seg_end[SUBLANES - 1:SUBLANES, :], (SUBLANES, D_LRU))

        ya_s = []
        for tau in range(seg_len):
            h_t = h_loc[tau] + a_cum[tau] * seg_entry
            g_t = ga_s[tau]
            gelu = 0.5 * g_t * (1.0 + jnp.tanh(math.sqrt(2.0 / math.pi) * (g_t + 0.044715 * (g_t * g_t * g_t))))
            ya_s.append(h_t * gelu)
        ya = _from_segments(seg_scr.at[s_i, 2], ya_s, seg_len, pitch)

        vn = _layer_norm(vs, sgg_ref[...], sgb_ref[...])
        if emit_vn:
            vn_ref[r0:r0 + sub, :] = vn
        vnb = vn.astype(BF16)
        if sub < SG_LEN:
            vnb = jnp.concatenate([vnb, jnp.zeros((SG_LEN - sub, D_SG), BF16)], axis=0)
        z_chunks = []
        for c in range(max(sub // SG_LEN, 1)):
            vc = vnb[c * SG_LEN:(c + 1) * SG_LEN, :]
            zc = jnp.concatenate(
                [_dot(w_causal[g], vc[:, g * SG_GROUP_DIM:(g + 1) * SG_GROUP_DIM]) for g in range(SG_GROUPS)],
                axis=1) + sgz_ref[...]
            z_chunks.append(zc[0:min(sub, SG_LEN), :])
        z = z_chunks[0] if len(z_chunks) == 1 else jnp.concatenate(z_chunks, axis=0)
        yb = us * z

        m = _dot(jnp.concatenate([ya, yb], axis=1).astype(BF16), wout_ref[...])
        y = ALPHA * x + (1.0 + gate) * m
        y_ref[r0:r0 + sub, :] = _layer_norm(y, lnp_ref[0:1, :], lnp_ref[1:2, :])

    for d in range(1, CONV_W):
        conv_carry[d - 1] = prev_tail[d - 1]
    h_carry[...] = hb
    hlast_ref[...] = hb[0:1, :]
    hist_ref[...] = xa[sub - (CONV_W - 1):sub, :]


def _mixer_call(x, mod, lnp, win, wout, convw, convb, wa, ba, wx, bx, lam, sgg, sgb, sgw, sgz,
                hist0, h0, *, tm, emit_vn):
    B, T, _ = x.shape
    per_batch = lambda b, j: (b, 0, 0)
    tile = lambda b, j: (b, j, 0)
    out_shape = [
        jax.ShapeDtypeStruct((B, T, D_MODEL), F32),
        jax.ShapeDtypeStruct((B, CONV_W - 1, D_LRU), F32),
        jax.ShapeDtypeStruct((B, 1, D_LRU), F32),
    ]
    out_specs = [
        pl.BlockSpec((None, tm, D_MODEL), tile),
        pl.BlockSpec((None, CONV_W - 1, D_LRU), per_batch),
        pl.BlockSpec((None, 1, D_LRU), per_batch),
    ]
    if emit_vn:
        out_shape.append(jax.ShapeDtypeStruct((B, T, D_SG), F32))
        out_specs.append(pl.BlockSpec((None, tm, D_SG), tile))
    sub = _sub_rows(tm)
    seg_rows = SUBLANES * _seg_pitch(sub // SUBLANES)
    return pl.pallas_call(
        functools.partial(_mixer_kernel, tm=tm, emit_vn=emit_vn),
        grid=(B, T // tm),
        in_specs=[
            pl.BlockSpec((None, tm, D_MODEL), tile),
            pl.BlockSpec((None, N_MOD, D_MODEL), per_batch),
            _resident((2, D_MODEL)),
            _resident((D_MODEL, D_IN)),
            _resident((D_LRU + D_SG, D_MODEL)),
            _resident((CONV_W, D_LRU)),
            _resident((1, D_LRU)),
            _resident((D_LRU, D_LRU)),
            _resident((1, D_LRU)),
            _resident((D_LRU, D_LRU)),
            _resident((1, D_LRU)),
            _resident((1, D_LRU)),
            _resident((1, D_SG)),
            _resident((1, D_SG)),
            _resident((SG_GROUPS, SG_LEN, SG_LEN)),
            _resident((SG_LEN, D_SG)),
            pl.BlockSpec((None, CONV_W - 1, D_LRU), per_batch),
            pl.BlockSpec((None, 1, D_LRU), per_batch),
        ],
        out_specs=out_specs,
        out_shape=out_shape,
        scratch_shapes=[
            pltpu.VMEM((tm // sub, 3, D_LRU // LANES, seg_rows, LANES), F32),
            pltpu.VMEM((CONV_W - 1, SUBLANES, D_LRU), F32),
            pltpu.VMEM((SUBLANES, D_LRU), F32),
        ],
        compiler_params=pltpu.CompilerParams(
            dimension_semantics=("arbitrary", "arbitrary"),
            vmem_limit_bytes=VMEM_LIMIT_BYTES),
        name="rglru_gmlp_mixer",
    )(x, mod, lnp, win, wout, convw, convb, wa, ba, wx, bx, lam, sgg, sgb, sgw, sgz, hist0, h0)


def _block_diag(w):
    n_heads, d, _ = w.shape
    eye = jnp.eye(n_heads, dtype=w.dtype)
    return (eye[:, None, :, None] * w[:, :, None, :]).reshape(n_heads * d, n_heads * d)


def kernel(x_prompt, x_sample, c_prompt, c_sample, state_conv, state_lru, ln_in_g, ln_in_b, w_ada, b_ada, ln_g, ln_b, ffn1_w1, ffn1_w3, ffn1_w2, ffn2_w1, ffn2_w3, ffn2_w2, w_in, w_out, conv_w, conv_b, lru_wa, lru_ba, lru_wx, lru_bx, lru_lambda, sg_ln_g, sg_ln_b, sg_w, sg_b):
    Bp = x_prompt.shape[0]
    row = lambda v: v.reshape(1, -1)
    c_all = jnp.concatenate([c_prompt, c_sample], axis=0)

    def run(x, mods, hists, h0s, emit_vn):
        tm_ffn = _tile_rows(x.shape[1], FFN_TILE_ROWS)
        tm_mix = _tile_rows(x.shape[1], MIXER_TILE_ROWS)
        new_hists, new_hs, vns = [], [], []
        for l in range(DEPTH):
            mod = mods[l]
            lnp1 = jnp.stack([ln_in_g, ln_in_b, ln_g[l, 0], ln_b[l, 0]])
            lnp2 = jnp.stack([ln_g[l, 1], ln_b[l, 1]])
            lnp3 = jnp.stack([ln_g[l, 1], ln_b[l, 1], ln_g[l, 2], ln_b[l, 2]])
            x = _ffn_call(x, mod, lnp1, ffn1_w1[l].astype(BF16), ffn1_w3[l].astype(BF16),
                          ffn1_w2[l].astype(BF16), k=0, pre_ln=(l == 0), tm=tm_ffn)
            sgz = jnp.repeat(jnp.transpose(sg_b[l]), SG_GROUP_DIM, axis=1)
            outs = _mixer_call(
                x, mod, lnp2, w_in[l].astype(BF16), w_out[l].astype(BF16), conv_w[l], row(conv_b[l]),
                _block_diag(lru_wa[l]).astype(BF16), row(lru_ba[l]),
                _block_diag(lru_wx[l]).astype(BF16), row(lru_bx[l]), row(lru_lambda[l]),
                row(sg_ln_g[l]), row(sg_ln_b[l]), sg_w[l], sgz,
                hists[l], h0s[l][:, None, :], tm=tm_mix, emit_vn=emit_vn)
            x = outs[0]
            new_hists.append(outs[1])
            new_hs.append(outs[2][:, 0, :])
            if emit_vn:
                vns.append(outs[3])
            x = _ffn_call(x, mod, lnp3, ffn2_w1[l].astype(BF16), ffn2_w3[l].astype(BF16),
                          ffn2_w2[l].astype(BF16), k=2, pre_ln=False, tm=tm_ffn)
        return x, jnp.stack(new_hists), jnp.stack(new_hs), (jnp.stack(vns) if emit_vn else None)

    mods = [_mod_call(c_all, w_ada[l], b_ada[l]).reshape(-1, N_MOD, D_MODEL) for l in range(DEPTH)]
    mods_p = [m[:Bp] for m in mods]
    mods_s = [m[Bp:] for m in mods]
    zero_hist = jnp.zeros((DEPTH, Bp, CONV_W - 1, D_LRU), x_prompt.dtype)
    zero_h = jnp.zeros((DEPTH, Bp, D_LRU), x_prompt.dtype)
    y_prompt, conv_p, lru_p, _ = run(x_prompt, mods_p, zero_hist, zero_h, False)
    y_sample, conv_s, lru_s, v_s = run(x_sample, mods_s, state_conv, state_lru, True)
    return (y_prompt, y_sample, conv_p, lru_p, conv_s, lru_s, v_s)
```
